```python
import math
import jax, jax.numpy as jnp
from jax import lax
import numpy as np

D_MODEL = 2048
BATCH = 8
SEQ = 2048
DEPTH = 1

ML_HEADS = 8
ML_QK_DIM = D_MODEL // 2 // ML_HEADS
ML_V_DIM = D_MODEL // ML_HEADS
ML_QK_WIDTH = ML_HEADS * ML_QK_DIM
ML_WIDTH = ML_HEADS * ML_V_DIM
GATE_SOFTCAP = 15.0
DN_HEADS = 16
DN_HEAD_DIM = D_MODEL // DN_HEADS
DN_WIDTH = DN_HEADS * DN_HEAD_DIM
CONV_WIDTH = 4
CHUNK = 64
NORM_EPS = 1e-6

IN_SIZES = (ML_QK_WIDTH, ML_QK_WIDTH, ML_WIDTH, ML_WIDTH, ML_WIDTH, ML_HEADS, ML_HEADS,
            3 * DN_WIDTH, DN_WIDTH, DN_HEADS, DN_HEADS,
            D_MODEL, D_MODEL)
IN_DIM = sum(IN_SIZES)
OUT_DIM = ML_WIDTH + DN_WIDTH

kernel_name = "hybrid_mlstm_gdn_gated_merge"


def _rmsnorm(x, w=None):
    xf = x.astype(jnp.float32)
    y = xf * lax.rsqrt(jnp.mean(xf * xf, axis=-1, keepdims=True) + NORM_EPS)
    return y if w is None else y * w.astype(jnp.float32)


def _l2norm(x):
    return x * lax.rsqrt(jnp.sum(x * x, axis=-1, keepdims=True) + NORM_EPS)


def _heads(t, n):
    b, s, _ = t.shape
    return t.reshape(b, s, n, -1).transpose(0, 2, 1, 3)


def _merge_heads(t):
    b, h, s, d = t.shape
    return t.transpose(0, 2, 1, 3).reshape(b, s, h * d)


def _chunks(t):
    b, h, s = t.shape[:3]
    t = t.reshape(b, h, s // CHUNK, CHUNK, *t.shape[3:])
    return jnp.moveaxis(t, 2, 0)


def _unchunk(t):
    nc, b, h, l, d = t.shape
    return jnp.moveaxis(t, 0, 2).reshape(b, h, nc * l, d)


def _softcap(t):
    return GATE_SOFTCAP * jnp.tanh(t / GATE_SOFTCAP)


def _mlstm_chunkwise(q, k, v, i_pre, logf):
    b_, h_, s_, dk = q.shape
    dv = v.shape[-1]
    q = q * (dk ** -0.5)
    qc, kc, vc = _chunks(q), _chunks(k), _chunks(v)
    ic, fc = _chunks(i_pre), _chunks(logf)
    bcum = jnp.cumsum(fc, axis=-1)
    causal = jnp.tril(jnp.ones((CHUNK, CHUNK), dtype=bool))
    dmat = bcum[..., :, None] - bcum[..., None, :] + ic[..., None, :]
    dmat = jnp.where(causal, dmat, -jnp.inf)
    dmax = jnp.max(dmat, axis=-1)
    scores = jnp.einsum('cbhtd,cbhsd->cbhts', qc, kc)
    w_end = bcum[..., -1:] - bcum + ic
    w_end_max = jnp.max(w_end, axis=-1)

    def step(carry, inp):
        C, n, m = carry
        q_c, k_c, v_c, b_c, dm_c, dmax_c, sc_c, we_c, wem_c = inp
        inter_log = b_c + m[..., None]
        m_t = jnp.maximum(inter_log, dmax_c)
        inter_w = jnp.exp(inter_log - m_t)
        intra_w = jnp.exp(dm_c - m_t[..., None]) * sc_c
        num = (inter_w[..., None] * jnp.einsum('bhtd,bhde->bhte', q_c, C)
               + jnp.einsum('bhts,bhse->bhte', intra_w, v_c))
        den = inter_w * jnp.einsum('bhtd,bhd->bht', q_c, n) + jnp.sum(intra_w, axis=-1)
        h = num / jnp.maximum(jnp.abs(den), jnp.exp(-m_t))[..., None]
        m_new = jnp.maximum(b_c[..., -1] + m, wem_c)
        decay = jnp.exp(b_c[..., -1] + m - m_new)
        kw = k_c * jnp.exp(we_c - m_new[..., None])[..., None]
        C_new = decay[..., None, None] * C + jnp.einsum('bhsd,bhse->bhde', kw, v_c)
        n_new = decay[..., None] * n + jnp.sum(kw, axis=-2)
        return (C_new, n_new, m_new), h

    init = (jnp.zeros((b_, h_, dk, dv), jnp.float32),
            jnp.zeros((b_, h_, dk), jnp.float32),
            jnp.zeros((b_, h_), jnp.float32))
    _, hs = lax.scan(step, init, (qc, kc, vc, bcum, dmat, dmax, scores, w_end, w_end_max))
    return _unchunk(hs)


def _gated_delta_chunkwise(q, k, v, beta, g):
    b_, h_, s_, dk = q.shape
    dv = v.shape[-1]
    q = q * (dk ** -0.5)
    qc, kc, vc = _chunks(q), _chunks(k), _chunks(v)
    bc, gc = _chunks(beta), _chunks(g)
    gam = jnp.cumsum(gc, axis=-1)
    incl = jnp.tril(jnp.ones((CHUNK, CHUNK), dtype=bool))
    strict = jnp.tril(jnp.ones((CHUNK, CHUNK), dtype=bool), k=-1)
    rel = gam[..., :, None] - gam[..., None, :]
    decay_mat = jnp.where(incl, jnp.exp(jnp.where(incl, rel, 0.0)), 0.0)
    kk = jnp.einsum('cbhtd,cbhsd->cbhts', kc, kc)
    a_mat = jnp.where(strict, bc[..., :, None] * kk * decay_mat, 0.0)
    lhs = a_mat + jnp.eye(CHUNK, dtype=a_mat.dtype)
    rhs = jnp.concatenate([vc * bc[..., None], kc * (bc * jnp.exp(gam))[..., None]], axis=-1)
    sol = lax.linalg.triangular_solve(lhs, rhs, left_side=True, lower=True, unit_diagonal=True)
    w_val, w_k = sol[..., :dv], sol[..., dv:]
    qk = jnp.einsum('cbhtd,cbhsd->cbhts', qc, kc) * decay_mat
    q_dec = qc * jnp.exp(gam)[..., None]
    k_end = kc * jnp.exp(gam[..., -1:] - gam)[..., None]
    g_end = jnp.exp(gam[..., -1])

    def step(S, inp):
        wv, wk, qk_c, qd, ke, ge = inp
        u = wv - jnp.einsum('bhtd,bhde->bhte', wk, S)
        o = jnp.einsum('bhtd,bhde->bhte', qd, S) + jnp.einsum('bhts,bhse->bhte', qk_c, u)
        S_new = ge[..., None, None] * S + jnp.einsum('bhsd,bhse->bhde', ke, u)
        return S_new, o

    S0 = jnp.zeros((b_, h_, dk, dv), jnp.float32)
    _, os_ = lax.scan(step, S0, (w_val, w_k, qk, q_dec, k_end, g_end))
    return _unchunk(os_)


def _causal_conv_silu(x, w):
    kw = w.shape[0]
    s = x.shape[1]
    xp = jnp.pad(x, ((0, 0), (kw - 1, 0), (0, 0)))
    y = sum(w[j].astype(jnp.float32) * xp[:, j:j + s] for j in range(kw))
    return jax.nn.silu(y)


def _hybrid_layer(h, norm_w, w_in, ml_i_bias, ml_f_bias, ml_norm_w,
                  dn_conv_w, dn_a_log, dn_dt_bias, dn_norm_w, w_out):
    u = _rmsnorm(h, norm_w)
    proj = jnp.matmul(u, w_in.astype(jnp.float32))
    idx = np.cumsum(IN_SIZES)[:-1].tolist()
    (ml_q, ml_k, ml_v, ml_o, ml_z, ml_i, ml_f,
     dn_qkv, dn_z, dn_b, dn_a, gate_a, gate_b) = jnp.split(proj, idx, axis=-1)

    i_pre = _softcap(ml_i + ml_i_bias).transpose(0, 2, 1)
    logf = jax.nn.log_sigmoid(_softcap(ml_f + ml_f_bias)).transpose(0, 2, 1)
    h_a = _mlstm_chunkwise(_heads(ml_q, ML_HEADS), _heads(ml_k, ML_HEADS),
                           _heads(ml_v, ML_HEADS), i_pre, logf)
    h_a = _merge_heads(_rmsnorm(h_a)) * ml_norm_w.astype(jnp.float32)
    h_a = h_a * jax.nn.sigmoid(ml_o) * jax.nn.silu(ml_z)

    qkv = _causal_conv_silu(dn_qkv, dn_conv_w)
    dq, dk_, dv_ = jnp.split(qkv, 3, axis=-1)
    dq = _l2norm(_heads(dq, DN_HEADS))
    dk_ = _l2norm(_heads(dk_, DN_HEADS))
    dv_ = _heads(dv_, DN_HEADS)
    beta = jax.nn.sigmoid(dn_b).transpose(0, 2, 1)
    g = (-jnp.exp(dn_a_log.astype(jnp.float32))
         * jax.nn.softplus(dn_a + dn_dt_bias)).transpose(0, 2, 1)
    h_b = _gated_delta_chunkwise(dq, dk_, dv_, beta, g)
    h_b = _rmsnorm(h_b, dn_norm_w)
    h_b = _merge_heads(h_b) * jax.nn.silu(dn_z)

    w_out = w_out.astype(jnp.float32)
    y_a = jnp.matmul(h_a, w_out[:ML_WIDTH])
    y_b = jnp.matmul(h_b, w_out[ML_WIDTH:])
    return jax.nn.sigmoid(gate_a) * y_a + jax.nn.sigmoid(gate_b) * y_b


def setup_inputs(seed: int = 0) -> dict:
    key = jax.random.key(seed)
    ks = jax.random.split(key, 13)
    x = jax.random.normal(ks[0], (BATCH, SEQ, D_MODEL), jnp.float32)
    norm_w = 1.0 + 0.05 * jax.random.normal(ks[1], (DEPTH, D_MODEL), jnp.float32)
    w_in = jax.random.normal(ks[2], (DEPTH, D_MODEL, IN_DIM), jnp.float32) * D_MODEL ** -0.5
    ml_i_bias = 0.1 * jax.random.normal(ks[3], (DEPTH, ML_HEADS), jnp.float32)
    ml_f_bias = 3.0 + 0.5 * jax.random.normal(ks[4], (DEPTH, ML_HEADS), jnp.float32)
    ml_norm_w = 1.0 + 0.05 * jax.random.normal(ks[5], (DEPTH, ML_WIDTH), jnp.float32)
    dn_conv_w = jax.random.normal(ks[6], (DEPTH, CONV_WIDTH, 3 * DN_WIDTH), jnp.float32) * CONV_WIDTH ** -0.5
    dn_a_log = jnp.log(jax.random.uniform(ks[7], (DEPTH, DN_HEADS), jnp.float32, 1.0, 16.0))
    dt = jnp.exp(jax.random.uniform(ks[8], (DEPTH, DN_HEADS), jnp.float32,
                                    math.log(1e-3), math.log(1e-1)))
    dn_dt_bias = dt + jnp.log(-jnp.expm1(-dt))
    dn_norm_w = 1.0 + 0.05 * jax.random.normal(ks[9], (DEPTH, DN_HEAD_DIM), jnp.float32)
    w_out = jax.random.normal(ks[10], (DEPTH, OUT_DIM, D_MODEL), jnp.float32) * OUT_DIM ** -0.5
    final_norm_w = 1.0 + 0.05 * jax.random.normal(ks[11], (D_MODEL,), jnp.float32)
    return {"x": x, "norm_w": norm_w, "w_in": w_in, "ml_i_bias": ml_i_bias,
            "ml_f_bias": ml_f_bias, "ml_norm_w": ml_norm_w, "dn_conv_w": dn_conv_w,
            "dn_a_log": dn_a_log, "dn_dt_bias": dn_dt_bias, "dn_norm_w": dn_norm_w,
            "w_out": w_out, "final_norm_w": final_norm_w}


def reference(x, norm_w, w_in, ml_i_bias, ml_f_bias, ml_norm_w, dn_conv_w,
              dn_a_log, dn_dt_bias, dn_norm_w, w_out, final_norm_w):
    h = x
    for l in range(DEPTH):
        y = _hybrid_layer(h, norm_w[l], w_in[l], ml_i_bias[l], ml_f_bias[l], ml_norm_w[l],
                          dn_conv_w[l], dn_a_log[l], dn_dt_bias[l], dn_norm_w[l], w_out[l])
        h = h + y.astype(h.dtype)
    return _rmsnorm(h, final_norm_w).astype(x.dtype)
```

```python
import functools

import jax
import jax.numpy as jnp
from jax import lax
from jax.experimental import pallas as pl
from jax.experimental.pallas import tpu as pltpu

F32 = jnp.float32
BF16 = jnp.bfloat16
HIGHEST = lax.Precision.HIGHEST

NORM_EPS = 1e-6
GATE_SOFTCAP = 15.0
CHUNK = 64
INV_BLOCK = 16
LANES = 128
GATE_PAD = LANES
NEG_BIG = -1e30
VMEM_LIMIT = 56 * 1024 * 1024


def _sigmoid(x):
    return 1.0 / (1.0 + jnp.exp(-x))


def _silu(x):
    return x * _sigmoid(x)


def _softplus(x):
    return jnp.maximum(x, 0.0) + jnp.log(1.0 + jnp.exp(-jnp.abs(x)))


def _log_sigmoid(x):
    return -_softplus(-x)


def _softcap(x):
    return GATE_SOFTCAP * jnp.tanh(x / GATE_SOFTCAP)


def _mm(a, b):
    return jnp.dot(a.astype(BF16), b.astype(BF16), preferred_element_type=F32)


def _mm_nt(a, b):
    return lax.dot_general(a.astype(BF16), b.astype(BF16), (((1,), (1,)), ((), ())),
                           preferred_element_type=F32)


def _mm_tn(a, b):
    return lax.dot_general(a.astype(BF16), b.astype(BF16), (((0,), (0,)), ((), ())),
                           preferred_element_type=F32)


def _iota2(shape, dim):
    return lax.broadcasted_iota(jnp.int32, shape, dim)


def _cumsum_rows(x_rows):
    l = x_rows.shape[-1]
    upper = (_iota2((l, l), 0) <= _iota2((l, l), 1)).astype(F32)
    return jnp.dot(x_rows, upper, precision=HIGHEST, preferred_element_type=F32)


def _cumsum_cols(x_cols):
    l = x_cols.shape[0]
    lower = (_iota2((l, l), 1) <= _iota2((l, l), 0)).astype(F32)
    return jnp.dot(lower, x_cols, precision=HIGHEST, preferred_element_type=F32)


def _pick_col(mat, c):
    sel = _iota2(mat.shape, 1) == c
    return jnp.sum(jnp.where(sel, mat, 0.0), axis=1, keepdims=True)


def _inproj_kernel(x_ref, nw_ref, w_ref, wg_ref, o_ref, g_ref, u_ref, *, sub_rows):
    tm = x_ref.shape[0]

    @pl.when(pl.program_id(1) == 0)
    def _():
        def body(r, carry):
            sl = pl.ds(pl.multiple_of(r * sub_rows, sub_rows), sub_rows)
            x = x_ref[sl, :]
            ms = jnp.mean(x * x, axis=-1, keepdims=True)
            u = x * lax.rsqrt(ms + NORM_EPS) * nw_ref[...]
            u_ref[sl, :] = u.astype(BF16)
            g_ref[sl, :] = jnp.dot(u, wg_ref[...], precision=HIGHEST,
                                   preferred_element_type=F32)
            return carry

        lax.fori_loop(0, tm // sub_rows, body, 0)

    o_ref[...] = jnp.dot(u_ref[...], w_ref[...], preferred_element_type=F32).astype(BF16)


def _inproj(x2, norm_w, w_main, w_gate, *, tm, tn):
    m, d = x2.shape
    n = w_main.shape[1]
    assert m % tm == 0 and n % tn == 0
    return pl.pallas_call(
        functools.partial(_inproj_kernel, sub_rows=128),
        grid=(m // tm, n // tn),
        in_specs=[
            pl.BlockSpec((tm, d), lambda i, j: (i, 0)),
            pl.BlockSpec((1, d), lambda i, j: (0, 0)),
            pl.BlockSpec((d, tn), lambda i, j: (0, j)),
            pl.BlockSpec((d, GATE_PAD), lambda i, j: (0, 0)),
        ],
        out_specs=[
            pl.BlockSpec((tm, tn), lambda i, j: (i, j)),
            pl.BlockSpec((tm, GATE_PAD), lambda i, j: (i, 0)),
        ],
        out_shape=[
            jax.ShapeDtypeStruct((m, n), BF16),
            jax.ShapeDtypeStruct((m, GATE_PAD), F32),
        ],
        scratch_shapes=[pltpu.VMEM((tm, d), BF16)],
        compiler_params=pltpu.CompilerParams(
            dimension_semantics=("parallel", "arbitrary"),
            vmem_limit_bytes=VMEM_LIMIT),
        name="inproj",
    )(x2, norm_w.reshape(1, d), w_main, w_gate)


def _mlstm_kernel(ib_ref, fb_ref, q_ref, k_ref, v_ref, o_ref, z_ref,
                  ir_ref, fr_ref, ic_ref, fc_ref, nw_ref, out_ref,
                  c_ref, n_ref, m_ref, br_ref, ar_ref, bc_ref, wc_ref, *, chunk):
    h = pl.program_id(1)
    seq, dk = q_ref.shape
    dv = v_ref.shape[1]
    nc = seq // chunk
    scale = dk ** -0.5
    ib = ib_ref[h]
    fb = fb_ref[h]

    i_r = _softcap(ir_ref[...] + ib)
    lf_r = _log_sigmoid(_softcap(fr_ref[...] + fb))
    i_c = _softcap(ic_ref[...] + ib)
    lf_c = _log_sigmoid(_softcap(fc_ref[...] + fb))
    b_r = _cumsum_rows(lf_r)
    b_c = _cumsum_cols(lf_c)
    br_ref[...] = b_r
    ar_ref[...] = i_r - b_r
    bc_ref[...] = b_c
    wc_ref[...] = b_c[chunk - 1:chunk, :] - b_c + i_c

    c_ref[...] = jnp.zeros_like(c_ref)
    n_ref[...] = jnp.zeros_like(n_ref)
    m_ref[...] = jnp.zeros_like(m_ref)

    causal = _iota2((chunk, chunk), 1) <= _iota2((chunk, chunk), 0)

    def body(c, carry):
        rows = pl.ds(pl.multiple_of(c * chunk, chunk), chunk)
        qc = q_ref[rows, :]
        kc = k_ref[rows, :]
        vc = v_ref[rows, :]
        b_col = _pick_col(bc_ref[...], c)
        w_col = _pick_col(wc_ref[...], c)
        a_row = ar_ref[pl.ds(c, 1), :]
        b_row = br_ref[pl.ds(c, 1), :]
        b_last = b_row[:, chunk - 1:chunk]
        m_old = m_ref[...]
        c_old = c_ref[...]
        n_old = n_ref[...]

        scores = _mm_nt(qc, kc)
        dmat = jnp.where(causal, b_col + a_row, NEG_BIG)
        dmax = jnp.max(dmat, axis=1, keepdims=True)
        inter_log = b_col + m_old
        m_t = jnp.maximum(inter_log, dmax)
        inter_w = jnp.exp(inter_log - m_t) * scale
        intra_w = jnp.exp(dmat - m_t) * (scores * scale)
        qf = qc.astype(F32)
        num = inter_w * _mm(qc, c_old) + _mm(intra_w, vc)
        den = (inter_w * jnp.sum(qf * n_old, axis=1, keepdims=True)
               + jnp.sum(intra_w, axis=1, keepdims=True))
        hh = num / jnp.maximum(jnp.abs(den), jnp.exp(-m_t))

        ms = jnp.mean(hh * hh, axis=1, keepdims=True)
        y = hh * lax.rsqrt(ms + NORM_EPS) * nw_ref[...]
        y = y * _sigmoid(o_ref[rows, :].astype(F32)) * _silu(z_ref[rows, :].astype(F32))
        out_ref[rows, :] = y.astype(out_ref.dtype)

        w_end_max = jnp.max(b_last + a_row, axis=1, keepdims=True)
        m_new = jnp.maximum(b_last + m_old, w_end_max)
        decay = jnp.exp(b_last + m_old - m_new)
        kw = kc.astype(F32) * jnp.exp(w_col - m_new)
        c_ref[...] = decay * c_old + _mm_tn(kw, vc)
        n_ref[...] = decay * n_old + jnp.sum(kw, axis=0, keepdims=True)
        m_ref[...] = m_new
        return carry

    lax.fori_loop(0, nc, body, 0)


def _mlstm(proj3, ir, fr, ic, fc, i_bias, f_bias, norm_w, *, heads, dk, dv, chunk):
    b, s, _ = proj3.shape
    nc = s // chunk
    qk_blocks = heads
    v_off = 2 * heads * dk // dv
    smem = pl.BlockSpec(memory_space=pltpu.SMEM)
    rows_spec = pl.BlockSpec((None, None, nc, chunk), lambda bi, hi: (bi, hi, 0, 0))
    cols_spec = pl.BlockSpec((None, None, chunk, nc), lambda bi, hi: (bi, hi, 0, 0))
    return pl.pallas_call(
        functools.partial(_mlstm_kernel, chunk=chunk),
        grid=(b, heads),
        in_specs=[
            smem, smem,
            pl.BlockSpec((None, s, dk), lambda bi, hi: (bi, 0, hi)),
            pl.BlockSpec((None, s, dk), lambda bi, hi: (bi, 0, qk_blocks + hi)),
            pl.BlockSpec((None, s, dv), lambda bi, hi: (bi, 0, v_off + hi)),
            pl.BlockSpec((None, s, dv), lambda bi, hi: (bi, 0, v_off + heads + hi)),
            pl.BlockSpec((None, s, dv), lambda bi, hi: (bi, 0, v_off + 2 * heads + hi)),
            rows_spec, rows_spec, cols_spec, cols_spec,
            pl.BlockSpec((1, dv), lambda bi, hi: (0, hi)),
        ],
        out_specs=pl.BlockSpec((None, s, dv), lambda bi, hi: (bi, 0, hi)),
        out_shape=jax.ShapeDtypeStruct((b, s, heads * dv), BF16),
        scratch_shapes=[
            pltpu.VMEM((dk, dv), F32), pltpu.VMEM((1, dk), F32), pltpu.VMEM((1, 1), F32),
            pltpu.VMEM((nc, chunk), F32), pltpu.VMEM((nc, chunk), F32),
            pltpu.VMEM((chunk, nc), F32), pltpu.VMEM((chunk, nc), F32),
        ],
        compiler_params=pltpu.CompilerParams(
            dimension_semantics=("parallel", "parallel"),
            vmem_limit_bytes=VMEM_LIMIT),
        name="mlstm",
    )(i_bias, f_bias, proj3, proj3, proj3, proj3, proj3, ir, fr, ic, fc,
      norm_w.reshape(1, heads * dv))


def _inv_unit_lower_minus_eye(a):
    l = a.shape[0]
    r = _iota2((l, l), 0)
    c = _iota2((l, l), 1)
    blk_shift = INV_BLOCK.bit_length() - 1
    assert 1 << blk_shift == INV_BLOCK and l % INV_BLOCK == 0
    same_blk = jnp.right_shift(r, blk_shift) == jnp.right_shift(c, blk_shift)
    eye = (r == c).astype(F32)
    d = jnp.where(same_blk, a, 0.0)
    lo = jnp.where(same_blk, 0.0, a)
    p = eye - d
    dp = _mm(d, d)
    order = 2
    while True:
        p = p + _mm(p, dp)
        order *= 2
        if order >= INV_BLOCK:
            break
        dp = _mm(dp, dp)
    e = _mm(p, lo)
    q = eye - e
    nblk = l // INV_BLOCK
    if nblk > 2:
        ep = _mm(e, e)
        order = 2
        while True:
            q = q + _mm(q, ep)
            order *= 2
            if order >= nblk:
                break
            ep = _mm(ep, ep)
    return _mm(q - eye, p) + (p - eye)


def _gdn_kernel(alog_ref, dtb_ref, q_ref, k_ref, v_ref, z_ref,
                ar_ref, bc_ref, ac_ref, cw_ref, nw_ref, out_ref,
                s_ref, gr_ref, gc_ref, btc_ref, wv_ref, wk_ref, qd_ref, ke_ref, qk_ref,
                *, chunk):
    h = pl.program_id(1)
    seq, dk = q_ref.shape
    nc = seq // chunk
    kw = cw_ref.shape[1]
    scale = dk ** -0.5
    neg_a = -jnp.exp(alog_ref[h])
    dtb = dtb_ref[h]

    g_r = neg_a * _softplus(ar_ref[...] + dtb)
    g_c = neg_a * _softplus(ac_ref[...] + dtb)
    gr_ref[...] = _cumsum_rows(g_r)
    gc_ref[...] = _cumsum_cols(g_c)
    btc_ref[...] = _sigmoid(bc_ref[...])

    incl = _iota2((chunk, chunk), 1) <= _iota2((chunk, chunk), 0)
    strict = _iota2((chunk, chunk), 1) < _iota2((chunk, chunk), 0)
    row8 = _iota2((8, dk), 0)

    def conv_silu(ref, which, c, rows):
        x = ref[rows, :].astype(F32)
        p0 = pl.multiple_of(jnp.maximum(c * chunk - 16, 0), 16)
        prev = ref[pl.ds(p0, 16), :].astype(F32)[8:16, :]
        prev = jnp.where(c > 0, prev, 0.0)
        w = cw_ref[which]
        y = w[kw - 1:kw, :] * x
        for j in range(1, kw):
            xr = pltpu.roll(x, j, axis=0)
            pr = pltpu.roll(prev, j, axis=0)
            head = jnp.where(row8 < j, pr, xr[0:8, :])
            xs = jnp.concatenate([head, xr[8:, :]], axis=0)
            y = y + w[kw - 1 - j:kw - j, :] * xs
        return _silu(y)

    def l2norm(x):
        return x * lax.rsqrt(jnp.sum(x * x, axis=1, keepdims=True) + NORM_EPS)

    def prep(c, carry):
        rows = pl.ds(pl.multiple_of(c * chunk, chunk), chunk)
        q = l2norm(conv_silu(q_ref, 0, c, rows))
        k = l2norm(conv_silu(k_ref, 1, c, rows))
        v = conv_silu(v_ref, 2, c, rows)
        gam_c = _pick_col(gc_ref[...], c)
        beta_c = _pick_col(btc_ref[...], c)
        gam_r = gr_ref[pl.ds(c, 1), :]
        gam_last = gam_r[:, chunk - 1:chunk]
        decay = jnp.where(incl, jnp.exp(jnp.where(incl, gam_c - gam_r, 0.0)), 0.0)
        kk = _mm_nt(k, k)
        a = jnp.where(strict, beta_c * kk * decay, 0.0)
        t_m = _inv_unit_lower_minus_eye(a)
        rhs = jnp.concatenate([v * beta_c, k * (beta_c * jnp.exp(gam_c))], axis=1)
        w = rhs + _mm(t_m, rhs)
        wv_ref[rows, :] = w[:, :dk]
        wk_ref[rows, :] = w[:, dk:].astype(BF16)
        qk_ref[c] = (_mm_nt(q, k) * scale * decay).astype(BF16)
        qd_ref[rows, :] = (q * (scale * jnp.exp(gam_c))).astype(BF16)
        ke_ref[rows, :] = (k * jnp.exp(gam_last - gam_c)).astype(BF16)
        return carry

    lax.fori_loop(0, nc, prep, 0)

    s_ref[...] = jnp.zeros_like(s_ref)

    def step(c, carry):
        rows = pl.ds(pl.multiple_of(c * chunk, chunk), chunk)
        s_old = s_ref[...]
        s_b = s_old.astype(BF16)
        g_end = jnp.exp(gr_ref[pl.ds(c, 1), :][:, chunk - 1:chunk])
        u = wv_ref[rows, :] - _mm(wk_ref[rows, :], s_b)
        o = _mm(qd_ref[rows, :], s_b) + _mm(qk_ref[c], u)
        s_ref[...] = g_end * s_old + _mm_tn(ke_ref[rows, :], u)
        ms = jnp.mean(o * o, axis=1, keepdims=True)
        y = o * lax.rsqrt(ms + NORM_EPS) * nw_ref[...]
        y = y * _silu(z_ref[rows, :].astype(F32))
        out_ref[rows, :] = y.astype(out_ref.dtype)
        return carry

    lax.fori_loop(0, nc, step, 0)


def _gdn(proj3, ar, bc, ac, a_log, dt_bias, conv_w3, norm_w, *, heads, dk, col0, chunk):
    b, s, _ = proj3.shape
    nc = s // chunk
    blk0 = col0 // dk
    smem = pl.BlockSpec(memory_space=pltpu.SMEM)
    rows_spec = pl.BlockSpec((None, None, nc, chunk), lambda bi, hi: (bi, hi, 0, 0))
    cols_spec = pl.BlockSpec((None, None, chunk, nc), lambda bi, hi: (bi, hi, 0, 0))
    kw = conv_w3.shape[1]
    return pl.pallas_call(
        functools.partial(_gdn_kernel, chunk=chunk),
        grid=(b, heads),
        in_specs=[
            smem, smem,
            pl.BlockSpec((None, s, dk), lambda bi, hi: (bi, 0, blk0 + hi)),
            pl.BlockSpec((None, s, dk), lambda bi, hi: (bi, 0, blk0 + heads + hi)),
            pl.BlockSpec((None, s, dk), lambda bi, hi: (bi, 0, blk0 + 2 * heads + hi)),
            pl.BlockSpec((None, s, dk), lambda bi, hi: (bi, 0, blk0 + 3 * heads + hi)),
            rows_spec, cols_spec, cols_spec,
            pl.BlockSpec((3, kw, dk), lambda bi, hi: (0, 0, hi)),
            pl.BlockSpec((1, dk), lambda bi, hi: (0, 0)),
        ],
        out_specs=pl.BlockSpec((None, s, dk), lambda bi, hi: (bi, 0, hi)),
        out_shape=jax.ShapeDtypeStruct((b, s, heads * dk), BF16),
        scratch_shapes=[
            pltpu.VMEM((dk, dk), F32),
            pltpu.VMEM((nc, chunk), F32), pltpu.VMEM((chunk, nc), F32),
            pltpu.VMEM((chunk, nc), F32),
            pltpu.VMEM((s, dk), F32),
            pltpu.VMEM((s, dk), BF16),
            pltpu.VMEM((s, dk), BF16),
            pltpu.VMEM((s, dk), BF16),
            pltpu.VMEM((nc, chunk, chunk), BF16),
        ],
        compiler_params=pltpu.CompilerParams(
            dimension_semantics=("parallel", "parallel"),
            vmem_limit_bytes=VMEM_LIMIT),
        name="gdn",
    )(a_log, dt_bias, proj3, proj3, proj3, proj3, ar, bc, ac, conv_w3,
      norm_w.reshape(1, dk))


def _outproj_kernel(x_ref, ha_ref, hb_ref, ga_ref, gb_ref, wa_ref, wb_ref, fw_ref, o_ref,
                    *, final_norm):
    ya = jnp.dot(ha_ref[...], wa_ref[...], preferred_element_type=F32)
    yb = jnp.dot(hb_ref[...], wb_ref[...], preferred_element_type=F32)
    y = (_sigmoid(ga_ref[...].astype(F32)) * ya + _sigmoid(gb_ref[...].astype(F32)) * yb)
    hnew = x_ref[...] + y
    if final_norm:
        ms = jnp.mean(hnew * hnew, axis=-1, keepdims=True)
        hnew = hnew * lax.rsqrt(ms + NORM_EPS) * fw_ref[...]
    o_ref[...] = hnew


def _outproj(x2, ha2, hb2, proj2, wa, wb, final_w, *, gate_col0, tm, final_norm):
    m, d = x2.shape
    ka = ha2.shape[1]
    kb = hb2.shape[1]
    ga_blk = gate_col0 // d
    resident = dict(pipeline_mode=pl.Buffered(1))
    return pl.pallas_call(
        functools.partial(_outproj_kernel, final_norm=final_norm),
        grid=(m // tm,),
        in_specs=[
            pl.BlockSpec((tm, d), lambda i: (i, 0)),
            pl.BlockSpec((tm, ka), lambda i: (i, 0)),
            pl.BlockSpec((tm, kb), lambda i: (i, 0)),
            pl.BlockSpec((tm, d), lambda i: (i, ga_blk)),
            pl.BlockSpec((tm, d), lambda i: (i, ga_blk + 1)),
            pl.BlockSpec((ka, d), lambda i: (0, 0), **resident),
            pl.BlockSpec((kb, d), lambda i: (0, 0), **resident),
            pl.BlockSpec((1, d), lambda i: (0, 0)),
        ],
        out_specs=pl.BlockSpec((tm, d), lambda i: (i, 0)),
        out_shape=jax.ShapeDtypeStruct((m, d), F32),
        compiler_params=pltpu.CompilerParams(
            dimension_semantics=("parallel",),
            vmem_limit_bytes=VMEM_LIMIT),
        name="outproj",
    )(x2, ha2, hb2, proj2, proj2, wa, wb, final_w.reshape(1, d))


def _gate_layouts(g, b, s, heads, chunk):
    rows = g.reshape(b, s, heads).transpose(0, 2, 1).reshape(b, heads, s // chunk, chunk)
    return rows, rows.transpose(0, 1, 3, 2)


def _layer(h, norm_w, w_in, ml_i_bias, ml_f_bias, ml_norm_w, dn_conv_w, dn_a_log,
           dn_dt_bias, dn_norm_w, w_out, final_w, final_norm):
    b, s, d = h.shape
    mlh = ml_i_bias.shape[0]
    dnh = dn_a_log.shape[0]
    mlw = ml_norm_w.shape[0]
    dnd = dn_norm_w.shape[0]
    dnw = dnh * dnd
    in_dim = w_in.shape[1]
    qkw = (in_dim - 3 * mlw - 2 * mlh - 4 * dnw - 2 * dnh - 2 * d) // 2
    dk = qkw // mlh
    dv = mlw // mlh

    sizes = (qkw, qkw, mlw, mlw, mlw, mlh, mlh, 3 * dnw, dnw, dnh, dnh, d, d)
    offs = [0]
    for sz in sizes:
        offs.append(offs[-1] + sz)
    seg = lambda i: w_in[:, offs[i]:offs[i + 1]]
    w_main = jnp.concatenate([seg(0), seg(1), seg(2), seg(3), seg(4), seg(7), seg(8),
                              seg(11), seg(12)], axis=1).astype(BF16)
    n_small = 2 * mlh + 2 * dnh
    w_gate = jnp.concatenate([seg(5), seg(6), seg(9), seg(10),
                              jnp.zeros((d, GATE_PAD - n_small), F32)], axis=1)
    dn_col0 = 2 * qkw + 3 * mlw
    gate_col0 = dn_col0 + 4 * dnw

    x2 = h.reshape(b * s, d)
    proj, gates = _inproj(x2, norm_w, w_main, w_gate, tm=min(1024, b * s), tn=512)
    proj3 = proj.reshape(b, s, proj.shape[1])

    ir, ic = _gate_layouts(gates[:, 0:mlh], b, s, mlh, CHUNK)
    fr, fc = _gate_layouts(gates[:, mlh:2 * mlh], b, s, mlh, CHUNK)
    _, bc = _gate_layouts(gates[:, 2 * mlh:2 * mlh + dnh], b, s, dnh, CHUNK)
    ar, ac = _gate_layouts(gates[:, 2 * mlh + dnh:2 * mlh + 2 * dnh], b, s, dnh, CHUNK)

    h_a = _mlstm(proj3, ir, fr, ic, fc, ml_i_bias, ml_f_bias, ml_norm_w,
                 heads=mlh, dk=dk, dv=dv, chunk=CHUNK)
    conv_w3 = dn_conv_w.reshape(dn_conv_w.shape[0], 3, dnw).transpose(1, 0, 2)
    h_b = _gdn(proj3, ar, bc, ac, dn_a_log, dn_dt_bias, conv_w3, dn_norm_w,
               heads=dnh, dk=dnd, col0=dn_col0, chunk=CHUNK)

    w_out_b = w_out.astype(BF16)
    out = _outproj(x2, h_a.reshape(b * s, mlw), h_b.reshape(b * s, dnw), proj,
                   w_out_b[:mlw], w_out_b[mlw:], final_w,
                   gate_col0=gate_col0, tm=256, final_norm=final_norm)
    return out.reshape(b, s, d)


def kernel(x, norm_w, w_in, ml_i_bias, ml_f_bias, ml_norm_w, dn_conv_w, dn_a_log, dn_dt_bias,
           dn_norm_w, w_out, final_norm_w):
    depth = norm_w.shape[0]
    h = x
    for l in range(depth):
        h = _layer(h, norm_w[l], w_in[l], ml_i_bias[l], ml_f_bias[l], ml_norm_w[l],
                   dn_conv_w[l], dn_a_log[l], dn_dt_bias[l], dn_norm_w[l], w_out[l],
                   final_norm_w, final_norm=(l == depth - 1))
    return h
```
